```python
import math
import jax, jax.numpy as jnp
from jax import lax
import numpy as np

D_MODEL = 1024
BATCH = 2
SEQ = 8192
DEPTH = 1
DEC_BATCH = 8
DEC_SEQ = 4096
PAST_LEN = 128

HEAD_DIM = 64
A_HEADS = 8
A_KV_HEADS = 2
A_GROUP = A_HEADS // A_KV_HEADS
A_Q_W = A_HEADS * HEAD_DIM
A_KV_W = A_KV_HEADS * HEAD_DIM
A_OUT_W = A_Q_W
B_HEADS = 4
B_QK_W = B_HEADS * 2 * HEAD_DIM
B_V_DIM = 2 * HEAD_DIM
B_V_W = B_HEADS * B_V_DIM
IN_W = A_Q_W + 2 * A_KV_W + 2 * B_QK_W + B_V_W
D_FF = 2816
GRID_W = 64
AXIAL_THETA = 10000.0
ROPE_THETA = 500000.0
ROPE_DIM = HEAD_DIM // 4
Q_BLOCK = 128
RMS_EPS = 1e-6
SCALE = 1.0 / math.sqrt(HEAD_DIM)

kernel_name = "hybrid_gqa_axial_diffattn_macaron_encoder"


def rms_norm(x, g):
    xf = x.astype(jnp.float32)
    y = xf * lax.rsqrt(jnp.mean(xf * xf, axis=-1, keepdims=True) + RMS_EPS)
    return (y * g.astype(jnp.float32)).astype(x.dtype)


def swiglu(h, w_gate, w_up, w_down):
    return (jax.nn.silu(h @ w_gate) * (h @ w_up)) @ w_down


def rotate(x, ang):
    m = ang.shape[-1]
    cos = jnp.cos(ang)[:, None, :]
    sin = jnp.sin(ang)[:, None, :]
    xf = x.astype(jnp.float32)
    x1, x2 = xf[..., :m], xf[..., m:]
    return jnp.concatenate([x1 * cos - x2 * sin, x2 * cos + x1 * sin], axis=-1).astype(x.dtype)


def axial_angles(seq_len):
    rows = seq_len // GRID_W
    row = jnp.repeat(jnp.arange(rows, dtype=jnp.float32), GRID_W)
    col = jnp.tile(jnp.arange(GRID_W, dtype=jnp.float32), rows)
    half = HEAD_DIM // 2
    inv = AXIAL_THETA ** (-jnp.arange(0, half, 2, dtype=jnp.float32) / half)
    return jnp.concatenate([row[:, None] * inv, col[:, None] * inv], axis=-1)


def partial_angles(seq_len):
    inv = ROPE_THETA ** (-jnp.arange(0, ROPE_DIM, 2, dtype=jnp.float32) / ROPE_DIM)
    pos = jnp.arange(seq_len, dtype=jnp.float32)
    return pos[:, None] * inv


def gqa_attention(q, k, v):
    bsz, seq_len = q.shape[0], q.shape[1]
    nb = seq_len // Q_BLOCK
    qb = q.reshape(bsz, nb, Q_BLOCK, A_KV_HEADS, A_GROUP, HEAD_DIM).transpose(1, 0, 2, 3, 4, 5)

    def block(qi):
        s = jnp.einsum('bqhgd,bkhd->bhgqk', qi, k).astype(jnp.float32) * SCALE
        p = jax.nn.softmax(s, axis=-1)
        return jnp.einsum('bhgqk,bkhd->bqhgd', p.astype(v.dtype), v)

    o = lax.map(block, qb)
    return o.transpose(1, 0, 2, 3, 4, 5).reshape(bsz, seq_len, A_OUT_W)


def diff_attention(q1, q2, k1, k2, v, lam):
    bsz, seq_len = q1.shape[0], q1.shape[1]
    nb = seq_len // Q_BLOCK

    def to_blocks(t):
        return t.reshape(bsz, nb, Q_BLOCK, B_HEADS, HEAD_DIM).transpose(1, 0, 2, 3, 4)

    def block(qs):
        a, b = qs
        s1 = jnp.einsum('bqhd,bkhd->bhqk', a, k1).astype(jnp.float32) * SCALE
        s2 = jnp.einsum('bqhd,bkhd->bhqk', b, k2).astype(jnp.float32) * SCALE
        p = jax.nn.softmax(s1, axis=-1) - lam * jax.nn.softmax(s2, axis=-1)
        return jnp.einsum('bhqk,bkhe->bqhe', p.astype(v.dtype), v)

    o = lax.map(block, (to_blocks(q1), to_blocks(q2)))
    return o.transpose(1, 0, 2, 3, 4).reshape(bsz, seq_len, B_HEADS, B_V_DIM)


def token_mix(h, w_in, w_branch_gate, a_q_norm, a_k_norm, b_q_norm, b_k_norm,
              b_lambda_q1, b_lambda_k1, b_lambda_q2, b_lambda_k2, b_out_norm,
              w_o_a, w_o_b, w_out, lam_init):
    bsz, seq_len, _ = h.shape
    proj = h @ w_in
    cuts = np.cumsum([A_Q_W, A_KV_W, A_KV_W, B_QK_W, B_QK_W]).tolist()
    qa, ka, va, qb, kb, vb = jnp.split(proj, cuts, axis=-1)

    ang_ax = axial_angles(seq_len)
    qa = rotate(rms_norm(qa.reshape(bsz, seq_len, A_HEADS, HEAD_DIM), a_q_norm), ang_ax)
    ka = rotate(rms_norm(ka.reshape(bsz, seq_len, A_KV_HEADS, HEAD_DIM), a_k_norm), ang_ax)
    va = va.reshape(bsz, seq_len, A_KV_HEADS, HEAD_DIM)
    oa = gqa_attention(qa, ka, va)

    ang_p = partial_angles(seq_len)

    def prep(t, g):
        t = rms_norm(t.reshape(bsz, seq_len, 2 * B_HEADS, HEAD_DIM), g)
        t = jnp.concatenate([rotate(t[..., :ROPE_DIM], ang_p), t[..., ROPE_DIM:]], axis=-1)
        t = t.reshape(bsz, seq_len, B_HEADS, 2, HEAD_DIM)
        return t[..., 0, :], t[..., 1, :]

    q1, q2 = prep(qb, b_q_norm)
    k1, k2 = prep(kb, b_k_norm)
    vb = vb.reshape(bsz, seq_len, B_HEADS, B_V_DIM)
    f32 = jnp.float32
    lam = (jnp.exp(jnp.sum(b_lambda_q1.astype(f32) * b_lambda_k1.astype(f32)))
           - jnp.exp(jnp.sum(b_lambda_q2.astype(f32) * b_lambda_k2.astype(f32))) + lam_init)
    ob = diff_attention(q1, q2, k1, k2, vb, lam)
    ob = (rms_norm(ob, b_out_norm) * (1.0 - lam_init)).reshape(bsz, seq_len, B_V_W)

    ya = oa @ w_o_a
    yb = ob @ w_o_b
    gates = jax.nn.sigmoid(h @ w_branch_gate)
    ga, gb = gates[..., :D_MODEL], gates[..., D_MODEL:]
    return (ga * ya + gb * yb) @ w_out


def run_trunk(x, ffn1_norm, ffn1_w_gate, ffn1_w_up, ffn1_w_down, mix_norm, w_in, w_branch_gate,
              a_q_norm, a_k_norm, b_q_norm, b_k_norm, b_lambda_q1, b_lambda_k1, b_lambda_q2,
              b_lambda_k2, b_out_norm, w_o_a, w_o_b, w_out,
              ffn2_norm, ffn2_w_gate, ffn2_w_up, ffn2_w_down):
    for l in range(DEPTH):
        lam_init = 0.8 - 0.6 * math.exp(-0.3 * l)
        x = x + 0.5 * swiglu(rms_norm(x, ffn1_norm[l]), ffn1_w_gate[l], ffn1_w_up[l], ffn1_w_down[l])
        h = rms_norm(x, mix_norm[l])
        x = x + token_mix(h, w_in[l], w_branch_gate[l], a_q_norm[l], a_k_norm[l], b_q_norm[l],
                          b_k_norm[l], b_lambda_q1[l], b_lambda_k1[l], b_lambda_q2[l],
                          b_lambda_k2[l], b_out_norm[l], w_o_a[l], w_o_b[l], w_out[l], lam_init)
        x = x + 0.5 * swiglu(rms_norm(x, ffn2_norm[l]), ffn2_w_gate[l], ffn2_w_up[l], ffn2_w_down[l])
    return x


def setup_inputs(seed: int = 0) -> dict:
    key = jax.random.key(seed)
    ks = jax.random.split(key, 32)
    f32 = jnp.float32

    def w(k, fan_in, fan_out):
        return jax.random.normal(k, (DEPTH, fan_in, fan_out), f32) * fan_in ** -0.5

    def gain(k, n):
        return 1.0 + 0.02 * jax.random.normal(k, (DEPTH, n), f32)

    def small(k, n):
        return 0.1 * jax.random.normal(k, (DEPTH, n), f32)

    return {
        "x_prompt": jax.random.normal(ks[0], (BATCH, SEQ, D_MODEL), f32),
        "x_sample": jax.random.normal(ks[1], (DEC_BATCH, DEC_SEQ, D_MODEL), f32),
        "ffn1_norm": gain(ks[2], D_MODEL),
        "ffn1_w_gate": w(ks[3], D_MODEL, D_FF),
        "ffn1_w_up": w(ks[4], D_MODEL, D_FF),
        "ffn1_w_down": w(ks[5], D_FF, D_MODEL),
        "mix_norm": gain(ks[6], D_MODEL),
        "w_in": w(ks[7], D_MODEL, IN_W),
        "w_branch_gate": w(ks[8], D_MODEL, 2 * D_MODEL),
        "a_q_norm": gain(ks[9], HEAD_DIM),
        "a_k_norm": gain(ks[10], HEAD_DIM),
        "b_q_norm": gain(ks[11], HEAD_DIM),
        "b_k_norm": gain(ks[12], HEAD_DIM),
        "b_lambda_q1": small(ks[13], HEAD_DIM),
        "b_lambda_k1": small(ks[14], HEAD_DIM),
        "b_lambda_q2": small(ks[15], HEAD_DIM),
        "b_lambda_k2": small(ks[16], HEAD_DIM),
        "b_out_norm": gain(ks[17], B_V_DIM),
        "w_o_a": w(ks[18], A_OUT_W, D_MODEL),
        "w_o_b": w(ks[19], B_V_W, D_MODEL),
        "w_out": w(ks[20], D_MODEL, D_MODEL),
        "ffn2_norm": gain(ks[21], D_MODEL),
        "ffn2_w_gate": w(ks[22], D_MODEL, D_FF),
        "ffn2_w_up": w(ks[23], D_MODEL, D_FF),
        "ffn2_w_down": w(ks[24], D_FF, D_MODEL),
    }


def reference(x_prompt, x_sample, ffn1_norm, ffn1_w_gate, ffn1_w_up, ffn1_w_down, mix_norm, w_in,
              w_branch_gate, a_q_norm, a_k_norm, b_q_norm, b_k_norm, b_lambda_q1, b_lambda_k1,
              b_lambda_q2, b_lambda_k2, b_out_norm, w_o_a, w_o_b, w_out,
              ffn2_norm, ffn2_w_gate, ffn2_w_up, ffn2_w_down):
    y_prompt = run_trunk(x_prompt, ffn1_norm, ffn1_w_gate, ffn1_w_up, ffn1_w_down, mix_norm, w_in,
                         w_branch_gate, a_q_norm, a_k_norm, b_q_norm, b_k_norm, b_lambda_q1,
                         b_lambda_k1, b_lambda_q2, b_lambda_k2, b_out_norm, w_o_a, w_o_b, w_out,
                         ffn2_norm, ffn2_w_gate, ffn2_w_up, ffn2_w_down)
    y_sample = run_trunk(x_sample, ffn1_norm, ffn1_w_gate, ffn1_w_up, ffn1_w_down, mix_norm, w_in,
                         w_branch_gate, a_q_norm, a_k_norm, b_q_norm, b_k_norm, b_lambda_q1,
                         b_lambda_k1, b_lambda_q2, b_lambda_k2, b_out_norm, w_o_a, w_o_b, w_out,
                         ffn2_norm, ffn2_w_gate, ffn2_w_up, ffn2_w_down)
    return (y_prompt, y_sample)
```

```python
import functools
import math

import jax
import jax.numpy as jnp
import numpy as np
from jax import lax
from jax.experimental import pallas as pl
from jax.experimental.pallas import tpu as pltpu

D_MODEL = 1024
HEAD_DIM = 64
A_HEADS = 8
A_KV_HEADS = 2
A_GROUP = A_HEADS // A_KV_HEADS
A_Q_W = A_HEADS * HEAD_DIM
A_KV_W = A_KV_HEADS * HEAD_DIM
B_HEADS = 4
B_QK_W = B_HEADS * 2 * HEAD_DIM
B_V_DIM = 2 * HEAD_DIM
B_V_W = B_HEADS * B_V_DIM
IN_W = A_Q_W + 2 * A_KV_W + 2 * B_QK_W + B_V_W
D_FF = 2816
GRID_W = 64
AXIAL_THETA = 10000.0
ROPE_THETA = 500000.0
ROPE_DIM = HEAD_DIM // 4
RMS_EPS = 1e-6
SCALE = 1.0 / math.sqrt(HEAD_DIM)
LAM_INIT = 0.8 - 0.6 * math.exp(-0.3 * 0)

LANES = 128
SEG_W = 256
OFF_KA = A_Q_W
OFF_VA = OFF_KA + A_KV_W
OFF_QB = OFF_VA + A_KV_W
OFF_KB = OFF_QB + B_QK_W
OFF_VB = OFF_KB + B_QK_W
VMEM_LIMIT_BYTES = 56 * 1024 * 1024

TOKEN_TILE = 512
KV_CHUNK = 512
A_Q_TILE = 128
B_Q_TILE = 512

_BF16 = jnp.bfloat16
_F32 = jnp.float32


def _rms(x, g):
    return (x * lax.rsqrt(jnp.mean(x * x, axis=-1, keepdims=True) + RMS_EPS)) * g


def _swiglu(xn, wg_ref, wu_ref, wd_ref):
    g = jnp.dot(xn, wg_ref[...], preferred_element_type=_F32)
    u = jnp.dot(xn, wu_ref[...], preferred_element_type=_F32)
    a = (g * jax.nn.sigmoid(g) * u).astype(_BF16)
    return jnp.dot(a, wd_ref[...], preferred_element_type=_F32)


def _rotate(y, cos, sin_lo, sin_hi, half):
    return (y * cos + pltpu.roll(y, LANES - half, 1) * sin_lo
            + pltpu.roll(y, half, 1) * sin_hi)


def _ffn_proj_body(x_ref, n1_ref, wg_ref, wu_ref, wd_ref, nm_ref, win_ref, seg_ref,
                   gain_ref, rot_ref, x1_ref, p_ref):
    x = x_ref[0]
    xn = _rms(x, n1_ref[...]).astype(_BF16)
    x1 = x + 0.5 * _swiglu(xn, wg_ref, wu_ref, wd_ref)
    x1_ref[0] = x1
    h = _rms(x1, nm_ref[...]).astype(_BF16)
    proj = jnp.dot(h, win_ref[...], preferred_element_type=_F32)

    rot_a = [rot_ref[:, i * LANES:(i + 1) * LANES] for i in range(3)]
    rot_b = [rot_ref[:, i * LANES:(i + 1) * LANES] for i in range(3, 6)]

    def norm_rot(off, tabs, half):
        xs = proj[:, off:off + SEG_W]
        ss = jnp.dot((xs * xs).astype(_BF16), seg_ref[...], preferred_element_type=_F32)
        y = xs * lax.rsqrt(ss * (1.0 / HEAD_DIM) + RMS_EPS) * gain_ref[:, off:off + SEG_W]
        return [_rotate(y[:, i * LANES:(i + 1) * LANES], *tabs, half)
                for i in range(SEG_W // LANES)]

    for off in range(0, A_Q_W, SEG_W):
        ys = norm_rot(off, rot_a, HEAD_DIM // 2)
        for i, yv in enumerate(ys):
            p_ref[0, :, off + i * LANES:off + (i + 1) * LANES] = yv.astype(_BF16)
    ka = norm_rot(OFF_KA, rot_a, HEAD_DIM // 2)[0]
    p_ref[0, :, OFF_KA:OFF_VA] = ka.astype(_BF16)
    p_ref[0, :, OFF_VA:OFF_QB] = proj[:, OFF_VA:OFF_QB].astype(_BF16)
    for off in range(OFF_QB, OFF_VB, SEG_W):
        ys = norm_rot(off, rot_b, ROPE_DIM // 2)
        for i, yv in enumerate(ys):
            p_ref[0, :, off + i * LANES:off + (i + 1) * LANES] = yv.astype(_BF16)
    p_ref[0, :, OFF_VB:IN_W] = proj[:, OFF_VB:IN_W].astype(_BF16)


def _mix_ffn_body(x1_ref, oa_ref, ob_ref, nm_ref, wbg_ref, woa_ref, wob_ref, wout_ref,
                  n2_ref, wg_ref, wu_ref, wd_ref, y_ref):
    x1 = x1_ref[0]
    h = _rms(x1, nm_ref[...]).astype(_BF16)
    gates = jax.nn.sigmoid(jnp.dot(h, wbg_ref[...], preferred_element_type=_F32))
    ya = jnp.dot(oa_ref[0], woa_ref[...], preferred_element_type=_F32)
    yb = jnp.dot(ob_ref[0], wob_ref[...], preferred_element_type=_F32)
    mix = (gates[:, :D_MODEL] * ya + gates[:, D_MODEL:] * yb).astype(_BF16)
    x2 = x1 + jnp.dot(mix, wout_ref[...], preferred_element_type=_F32)
    xn = _rms(x2, n2_ref[...]).astype(_BF16)
    y_ref[0] = x2 + 0.5 * _swiglu(xn, wg_ref, wu_ref, wd_ref)


def _flash_sweep(qs, k_ref, v_ref, m_sc, l_sc, acc_sc, kv_chunk):
    seq = k_ref.shape[1]
    m_sc[...] = jnp.full(m_sc.shape, -jnp.inf, _F32)
    l_sc[...] = jnp.zeros(l_sc.shape, _F32)
    acc_sc[...] = jnp.zeros(acc_sc.shape, _F32)

    def step(j, carry):
        start = pl.multiple_of(j * kv_chunk, kv_chunk)
        kc = k_ref[0, pl.ds(start, kv_chunk), :]
        vc = v_ref[0, pl.ds(start, kv_chunk), :]
        s = lax.dot_general(qs, kc, (((1,), (1,)), ((), ())), preferred_element_type=_F32)
        m_prev = m_sc[...]
        m_new = jnp.maximum(m_prev, jnp.max(s, axis=1, keepdims=True))
        alpha = jnp.exp(m_prev - m_new)
        p = jnp.exp(s - m_new)
        l_sc[...] = alpha * l_sc[...] + jnp.sum(p, axis=1, keepdims=True)
        acc_sc[...] = alpha * acc_sc[...] + jnp.dot(p.astype(_BF16), vc,
                                                    preferred_element_type=_F32)
        m_sc[...] = m_new
        return carry

    lax.fori_loop(0, seq // kv_chunk, step, 0)


def _attn_a_body(q_ref, k_ref, v_ref, o_ref, m_sc, l_sc, acc_sc, *, kv_chunk):
    tq = q_ref.shape[1]
    lane = lax.broadcasted_iota(jnp.int32, (tq, LANES), 1)
    rows = []
    for hd in range(A_HEADS):
        grp = hd // A_GROUP
        t = q_ref[0, :, (hd // 2) * LANES:(hd // 2 + 1) * LANES]
        if hd % 2 != grp:
            t = pltpu.roll(t, HEAD_DIM, 1)
        keep = (lane >= grp * HEAD_DIM) & (lane < (grp + 1) * HEAD_DIM)
        rows.append(jnp.where(keep, t, jnp.zeros_like(t)))
    qs = jnp.concatenate(rows, axis=0)
    _flash_sweep(qs, k_ref, v_ref, m_sc, l_sc, acc_sc, kv_chunk)
    o = acc_sc[...] / l_sc[...]
    for c in range(A_HEADS // 2):
        grp = (2 * c) // A_GROUP
        ev = o[(2 * c) * tq:(2 * c + 1) * tq]
        od = o[(2 * c + 1) * tq:(2 * c + 2) * tq]
        if grp == 0:
            od = pltpu.roll(od, HEAD_DIM, 1)
        else:
            ev = pltpu.roll(ev, HEAD_DIM, 1)
        o_ref[0, :, c * LANES:(c + 1) * LANES] = jnp.where(lane < HEAD_DIM, ev, od).astype(_BF16)


def _attn_b_body(q_ref, k_ref, v_ref, lam_ref, gout_ref, o_ref, m_sc, l_sc, acc_sc, *,
                 kv_chunk):
    tq = q_ref.shape[1]
    lane = lax.broadcasted_iota(jnp.int32, (tq, LANES), 1)
    qp = q_ref[0]
    zero = jnp.zeros_like(qp)
    qs = jnp.concatenate([jnp.where(lane < HEAD_DIM, qp, zero),
                          jnp.where(lane >= HEAD_DIM, qp, zero)], axis=0)
    _flash_sweep(qs, k_ref, v_ref, m_sc, l_sc, acc_sc, kv_chunk)
    lv = lam_ref[...]
    lam = (jnp.exp(jnp.sum(lv[0:1] * lv[1:2], axis=1, keepdims=True))
           - jnp.exp(jnp.sum(lv[2:3] * lv[3:4], axis=1, keepdims=True)) + LAM_INIT)
    o = acc_sc[...] / l_sc[...]
    ob = o[:tq] - lam * o[tq:]
    o_ref[0] = (_rms(ob, gout_ref[...]) * (1.0 - LAM_INIT)).astype(_BF16)


def _resident(shape):
    return pl.BlockSpec(shape, lambda *_: (0,) * len(shape), pipeline_mode=pl.Buffered(1))


def _rotary_tables(seq):
    d = np.arange(LANES) % HEAD_DIM
    rows = seq // GRID_W
    row = jnp.repeat(jnp.arange(rows, dtype=_F32), GRID_W)
    col = jnp.tile(jnp.arange(GRID_W, dtype=_F32), rows)
    half = HEAD_DIM // 2
    inv = AXIAL_THETA ** (-jnp.arange(0, half, 2, dtype=_F32) / half)
    ang_a = jnp.concatenate([row[:, None] * inv, col[:, None] * inv], axis=-1)
    ang_a = ang_a[:, d % half]
    lo_a = jnp.asarray(d < half)
    inv_b = ROPE_THETA ** (-jnp.arange(0, ROPE_DIM, 2, dtype=_F32) / ROPE_DIM)
    ang_b = (jnp.arange(seq, dtype=_F32)[:, None] * inv_b)[:, d % (ROPE_DIM // 2)]
    lo_b = jnp.asarray(d < ROPE_DIM // 2)
    hi_b = jnp.asarray((d >= ROPE_DIM // 2) & (d < ROPE_DIM))
    in_b = jnp.asarray(d < ROPE_DIM)
    zero = jnp.zeros((), _F32)
    return jnp.concatenate([
        jnp.cos(ang_a), jnp.where(lo_a, -jnp.sin(ang_a), zero), jnp.where(lo_a, zero, jnp.sin(ang_a)),
        jnp.where(in_b, jnp.cos(ang_b), 1.0), jnp.where(lo_b, -jnp.sin(ang_b), zero),
        jnp.where(hi_b, jnp.sin(ang_b), zero)], axis=-1)


def _trunk(x, w):
    bsz, seq, _ = x.shape
    tm = min(TOKEN_TILE, seq)
    tqa = min(A_Q_TILE, seq)
    tqb = min(B_Q_TILE, seq)
    tk = min(KV_CHUNK, seq)
    dense_params = pltpu.CompilerParams(
        dimension_semantics=("parallel", "parallel"), vmem_limit_bytes=VMEM_LIMIT_BYTES)
    tile_map = lambda b, i: (b, i, 0)
    row = lambda n: _resident((1, n))

    x1, proj = pl.pallas_call(
        _ffn_proj_body,
        grid=(bsz, seq // tm),
        in_specs=[pl.BlockSpec((1, tm, D_MODEL), tile_map), row(D_MODEL),
                  _resident((D_MODEL, D_FF)), _resident((D_MODEL, D_FF)),
                  _resident((D_FF, D_MODEL)), row(D_MODEL), _resident((D_MODEL, IN_W)),
                  _resident((SEG_W, SEG_W)), row(IN_W),
                  pl.BlockSpec((tm, 6 * LANES), lambda b, i: (i, 0))],
        out_specs=[pl.BlockSpec((1, tm, D_MODEL), tile_map),
                   pl.BlockSpec((1, tm, IN_W), tile_map)],
        out_shape=[jax.ShapeDtypeStruct((bsz, seq, D_MODEL), _F32),
                   jax.ShapeDtypeStruct((bsz, seq, IN_W), _BF16)],
        compiler_params=dense_params, name="ffn_proj",
    )(x, w["n1"], w["wg1"], w["wu1"], w["wd1"], w["nm"], w["win"], w["seg"], w["gain"],
      _rotary_tables(seq))

    def flash_scratch(m_rows):
        return [pltpu.VMEM((m_rows, 1), _F32), pltpu.VMEM((m_rows, 1), _F32),
                pltpu.VMEM((m_rows, LANES), _F32)]

    oa = pl.pallas_call(
        functools.partial(_attn_a_body, kv_chunk=tk),
        grid=(bsz, seq // tqa),
        in_specs=[pl.BlockSpec((1, tqa, A_Q_W), tile_map),
                  pl.BlockSpec((1, seq, LANES), lambda b, i: (b, 0, OFF_KA // LANES)),
                  pl.BlockSpec((1, seq, LANES), lambda b, i: (b, 0, OFF_VA // LANES))],
        out_specs=pl.BlockSpec((1, tqa, A_Q_W), tile_map),
        out_shape=jax.ShapeDtypeStruct((bsz, seq, A_Q_W), _BF16),
        scratch_shapes=flash_scratch(A_HEADS * tqa),
        compiler_params=pltpu.CompilerParams(
            dimension_semantics=("parallel", "parallel"), vmem_limit_bytes=VMEM_LIMIT_BYTES),
        name="attn_a",
    )(proj, proj, proj)

    ob = pl.pallas_call(
        functools.partial(_attn_b_body, kv_chunk=tk),
        grid=(bsz, B_HEADS, seq // tqb),
        in_specs=[pl.BlockSpec((1, tqb, LANES), lambda b, hd, i: (b, i, OFF_QB // LANES + hd)),
                  pl.BlockSpec((1, seq, LANES), lambda b, hd, i: (b, 0, OFF_KB // LANES + hd)),
                  pl.BlockSpec((1, seq, LANES), lambda b, hd, i: (b, 0, OFF_VB // LANES + hd)),
                  pl.BlockSpec((4, HEAD_DIM), lambda b, hd, i: (0, 0)),
                  pl.BlockSpec((1, B_V_DIM), lambda b, hd, i: (0, 0))],
        out_specs=pl.BlockSpec((1, tqb, LANES), lambda b, hd, i: (b, i, hd)),
        out_shape=jax.ShapeDtypeStruct((bsz, seq, B_V_W), _BF16),
        scratch_shapes=flash_scratch(2 * tqb),
        compiler_params=pltpu.CompilerParams(
            dimension_semantics=("parallel", "parallel", "parallel"),
            vmem_limit_bytes=VMEM_LIMIT_BYTES),
        name="attn_b",
    )(proj, proj, proj, w["lam"], w["gout"])

    return pl.pallas_call(
        _mix_ffn_body,
        grid=(bsz, seq // tm),
        in_specs=[pl.BlockSpec((1, tm, D_MODEL), tile_map),
                  pl.BlockSpec((1, tm, A_Q_W), tile_map),
                  pl.BlockSpec((1, tm, B_V_W), tile_map),
                  row(D_MODEL), _resident((D_MODEL, 2 * D_MODEL)),
                  _resident((A_Q_W, D_MODEL)), _resident((B_V_W, D_MODEL)),
                  _resident((D_MODEL, D_MODEL)), row(D_MODEL),
                  _resident((D_MODEL, D_FF)), _resident((D_MODEL, D_FF)),
                  _resident((D_FF, D_MODEL))],
        out_specs=pl.BlockSpec((1, tm, D_MODEL), tile_map),
        out_shape=jax.ShapeDtypeStruct((bsz, seq, D_MODEL), _F32),
        compiler_params=dense_params, name="mix_ffn",
    )(x1, oa, ob, w["nm"], w["wbg"], w["woa"], w["wob"], w["wout"], w["n2"],
      w["wg2"], w["wu2"], w["wd2"])


def _prepare_weights(ffn1_norm, ffn1_w_gate, ffn1_w_up, ffn1_w_down, mix_norm, w_in,
                     w_branch_gate, a_q_norm, a_k_norm, b_q_norm, b_k_norm, b_lambda_q1,
                     b_lambda_k1, b_lambda_q2, b_lambda_k2, b_out_norm, w_o_a, w_o_b, w_out,
                     ffn2_norm, ffn2_w_gate, ffn2_w_up, ffn2_w_down):
    bf = lambda t: t[0].astype(_BF16)
    row = lambda t: t[0].astype(_F32)[None, :]
    gain = jnp.concatenate([
        jnp.tile(a_q_norm[0] * SCALE, A_HEADS), jnp.tile(a_k_norm[0], A_KV_HEADS),
        jnp.ones((A_KV_W,), _F32), jnp.tile(b_q_norm[0] * SCALE, 2 * B_HEADS),
        jnp.tile(b_k_norm[0], 2 * B_HEADS), jnp.ones((B_V_W,), _F32)])[None, :]
    head = np.arange(SEG_W) // HEAD_DIM
    seg = jnp.asarray(head[:, None] == head[None, :], _BF16)
    return {
        "n1": row(ffn1_norm), "wg1": bf(ffn1_w_gate), "wu1": bf(ffn1_w_up),
        "wd1": bf(ffn1_w_down), "nm": row(mix_norm), "win": bf(w_in), "seg": seg,
        "gain": gain, "wbg": bf(w_branch_gate), "woa": bf(w_o_a), "wob": bf(w_o_b),
        "wout": bf(w_out), "n2": row(ffn2_norm), "wg2": bf(ffn2_w_gate),
        "wu2": bf(ffn2_w_up), "wd2": bf(ffn2_w_down), "gout": row(b_out_norm),
        "lam": jnp.stack([b_lambda_q1[0], b_lambda_k1[0], b_lambda_q2[0],
                          b_lambda_k2[0]]).astype(_F32),
    }


def kernel(x_prompt, x_sample, ffn1_norm, ffn1_w_gate, ffn1_w_up, ffn1_w_down, mix_norm, w_in, w_branch_gate, a_q_norm, a_k_norm, b_q_norm, b_k_norm, b_lambda_q1, b_lambda_k1, b_lambda_q2, b_lambda_k2, b_out_norm, w_o_a, w_o_b, w_out, ffn2_norm, ffn2_w_gate, ffn2_w_up, ffn2_w_down):
    w = _prepare_weights(ffn1_norm, ffn1_w_gate, ffn1_w_up, ffn1_w_down, mix_norm, w_in,
                         w_branch_gate, a_q_norm, a_k_norm, b_q_norm, b_k_norm, b_lambda_q1,
                         b_lambda_k1, b_lambda_q2, b_lambda_k2, b_out_norm, w_o_a, w_o_b,
                         w_out, ffn2_norm, ffn2_w_gate, ffn2_w_up, ffn2_w_down)
    return _trunk(x_prompt, w), _trunk(x_sample, w)
```
